```python
import jax, jax.numpy as jnp
from jax import lax
import numpy as np

D_MODEL = 1024
BATCH = 8
SEQ = 4096
DEPTH = 2

D_MIX = D_MODEL
D_A = D_MIX // 2
D_B = D_MIX - D_A
CHUNK = 128
H_A = 4
HD_A = D_A // H_A
H_B = 4
HD_B = D_B // H_B
CONV_W = 31
D_FF = 4 * D_MODEL
IN_COLS = 2 * D_A + 2 * D_B
EPS = 1e-6

kernel_name = "hybrid_sgu_conformer_conv_block"


def rms_norm(x, g):
    xf = x.astype(jnp.float32)
    y = xf * lax.rsqrt(jnp.mean(xf * xf, axis=-1, keepdims=True) + EPS)
    return (y * g.astype(jnp.float32)).astype(x.dtype)


def layer_norm(x, g, b):
    xf = x.astype(jnp.float32)
    mu = jnp.mean(xf, axis=-1, keepdims=True)
    var = jnp.mean(jnp.square(xf - mu), axis=-1, keepdims=True)
    y = (xf - mu) * lax.rsqrt(var + EPS)
    return (y * g.astype(jnp.float32) + b.astype(jnp.float32)).astype(x.dtype)


def spatial_gating(u_a, v_a, ln_g, ln_b, w_s, b_s):
    bsz, seq, _ = u_a.shape
    u = jax.nn.gelu(u_a, approximate=False)
    v = layer_norm(jax.nn.gelu(v_a, approximate=False), ln_g, ln_b)
    v = v.reshape(bsz, seq // CHUNK, CHUNK, H_A, HD_A)
    mask = jnp.tril(jnp.ones((CHUNK, CHUNK), dtype=w_s.dtype))
    w = w_s * mask[None]
    mixed = jnp.einsum('hts,bcshd->bcthd', w, v)
    mixed = mixed + jnp.transpose(b_s)[None, None, :, :, None]
    return u * mixed.reshape(bsz, seq, D_A)


def conformer_conv(val_b, gate_b, conv_w, conv_b, ln_g, ln_b):
    bsz, seq, _ = val_b.shape
    g = val_b * jax.nn.sigmoid(gate_b)
    c = lax.conv_general_dilated(
        g, conv_w[:, None, :].astype(g.dtype),
        window_strides=(1,), padding=[(CONV_W - 1, 0)],
        dimension_numbers=('NWC', 'WIO', 'NWC'),
        feature_group_count=D_B)
    c = c + conv_b
    c = layer_norm(c.reshape(bsz, seq, H_B, HD_B),
                   ln_g.reshape(H_B, HD_B), ln_b.reshape(H_B, HD_B))
    return jax.nn.silu(c).reshape(bsz, seq, D_B)


def setup_inputs(seed: int = 0) -> dict:
    key = jax.random.key(seed)
    ks = jax.random.split(key, 20)
    f32 = jnp.float32
    nrm = lambda k, shape, scale: jax.random.normal(k, shape, f32) * scale
    return {
        "x": jax.random.normal(ks[0], (BATCH, SEQ, D_MODEL), f32),
        "norm1_g": 1.0 + nrm(ks[1], (DEPTH, D_MODEL), 0.05),
        "w_in": nrm(ks[2], (DEPTH, D_MODEL, IN_COLS), D_MODEL ** -0.5),
        "sgu_ln_g": 1.0 + nrm(ks[3], (DEPTH, D_A), 0.05),
        "sgu_ln_b": nrm(ks[4], (DEPTH, D_A), 0.02),
        "sgu_w": nrm(ks[5], (DEPTH, H_A, CHUNK, CHUNK), CHUNK ** -0.5),
        "sgu_b": 1.0 + nrm(ks[6], (DEPTH, H_A, CHUNK), 0.1),
        "conv_w": nrm(ks[7], (DEPTH, CONV_W, D_B), CONV_W ** -0.5),
        "conv_b": nrm(ks[8], (DEPTH, D_B), 0.02),
        "conv_ln_g": 1.0 + nrm(ks[9], (DEPTH, D_B), 0.05),
        "conv_ln_b": nrm(ks[10], (DEPTH, D_B), 0.02),
        "w_out": nrm(ks[11], (DEPTH, D_MIX, D_MODEL), D_MIX ** -0.5),
        "norm2_g": 1.0 + nrm(ks[12], (DEPTH, D_MODEL), 0.05),
        "w_ff1": nrm(ks[13], (DEPTH, D_MODEL, D_FF), D_MODEL ** -0.5),
        "w_ff2": nrm(ks[14], (DEPTH, D_FF, D_MODEL), D_FF ** -0.5),
        "final_g": 1.0 + nrm(ks[15], (D_MODEL,), 0.05),
    }


def reference(x, norm1_g, w_in, sgu_ln_g, sgu_ln_b, sgu_w, sgu_b, conv_w, conv_b,
              conv_ln_g, conv_ln_b, w_out, norm2_g, w_ff1, w_ff2, final_g):
    for l in range(DEPTH):
        h = rms_norm(x, norm1_g[l])
        proj = jnp.einsum('bsd,dc->bsc', h, w_in[l])
        u_a = proj[..., :D_A]
        v_a = proj[..., D_A:2 * D_A]
        val_b = proj[..., 2 * D_A:2 * D_A + D_B]
        gate_b = proj[..., 2 * D_A + D_B:]
        a_out = spatial_gating(u_a, v_a, sgu_ln_g[l], sgu_ln_b[l], sgu_w[l], sgu_b[l])
        b_out = conformer_conv(val_b, gate_b, conv_w[l], conv_b[l],
                               conv_ln_g[l], conv_ln_b[l])
        mix = jnp.concatenate([a_out, b_out], axis=-1)
        x = x + jnp.einsum('bsc,cd->bsd', mix, w_out[l])
        h = rms_norm(x, norm2_g[l])
        f = jnp.square(jax.nn.relu(jnp.einsum('bsd,df->bsf', h, w_ff1[l])))
        x = x + jnp.einsum('bsf,fd->bsd', f, w_ff2[l])
    return rms_norm(x, final_g)
```

```python
import functools
import math

import jax
import jax.numpy as jnp
from jax import lax
from jax.experimental import pallas as pl
from jax.experimental.pallas import tpu as pltpu

D_MODEL = 1024
D_A = 512
D_B = 512
CHUNK = 128
H_A = 4
HD_A = D_A // H_A
H_B = 4
HD_B = D_B // H_B
CONV_W = 31
D_FF = 4 * D_MODEL
EPS = 1e-6

SEQ_TILE = 512
HALO = 32
CONV_ROWS = 64
MLP_TILE = 512
FF_CHUNK = 1024
VMEM_LIMIT_BYTES = 56 * 1024 * 1024

_SQRT_HALF = math.sqrt(0.5)


def _rms_norm(x, g):
    ms = jnp.mean(x * x, axis=-1, keepdims=True)
    return x * lax.rsqrt(ms + EPS) * g


def _layer_norm(x, g, b):
    mu = jnp.mean(x, axis=-1, keepdims=True)
    xc = x - mu
    var = jnp.mean(xc * xc, axis=-1, keepdims=True)
    return xc * lax.rsqrt(var + EPS) * g + b


def _gelu(x):
    return 0.5 * x * (1.0 + lax.erf(x * _SQRT_HALF))


def _mixer_kernel(x_ref, n1g_ref, win_ref, lng_ref, lnb_ref, sw_ref, sb_ref,
                  cw_ref, cb_ref, clg_ref, clb_ref, wout_ref, o_ref,
                  g_ref, mix_ref):
    @pl.when(pl.program_id(1) == 0)
    def _():
        g_ref[:, 0:HALO, :] = jnp.zeros((H_B, HALO, HD_B), jnp.float32)

    x = x_ref[0]
    h = _rms_norm(x, n1g_ref[...])
    proj = jnp.dot(h.astype(jnp.bfloat16), win_ref[...], preferred_element_type=jnp.float32)

    val = proj[:, 2 * D_A:2 * D_A + D_B]
    gate = proj[:, 2 * D_A + D_B:]
    g = val * jax.nn.sigmoid(gate)
    for grp in range(H_B):
        g_ref[grp, HALO:HALO + SEQ_TILE, :] = g[:, grp * HD_B:(grp + 1) * HD_B]

    for grp in range(H_B):
        lanes = slice(grp * HD_B, (grp + 1) * HD_B)
        taps = [cw_ref[k:k + 1, lanes] for k in range(CONV_W)]
        bias = cb_ref[:, lanes]
        ln_g = clg_ref[:, lanes]
        ln_b = clb_ref[:, lanes]

        def conv_rows(r, carry, grp=grp, taps=taps, bias=bias, ln_g=ln_g, ln_b=ln_b):
            base = pl.multiple_of(r * CONV_ROWS, CONV_ROWS)
            acc = jnp.broadcast_to(bias, (CONV_ROWS, HD_B))
            for k in range(CONV_W):
                start = base + (HALO - (CONV_W - 1) + k)
                acc = acc + taps[k] * g_ref[grp, pl.ds(start, CONV_ROWS), :]
            c = _layer_norm(acc, ln_g, ln_b)
            mix_ref[pl.ds(base, CONV_ROWS), D_A + grp * HD_B:D_A + (grp + 1) * HD_B] = (
                c * jax.nn.sigmoid(c)).astype(jnp.bfloat16)
            return carry

        lax.fori_loop(0, SEQ_TILE // CONV_ROWS, conv_rows, 0)

    g_ref[:, 0:HALO, :] = g_ref[:, SEQ_TILE:SEQ_TILE + HALO, :]

    u = _gelu(proj[:, :D_A])
    v = _layer_norm(_gelu(proj[:, D_A:2 * D_A]), lng_ref[...], lnb_ref[...]).astype(jnp.bfloat16)
    t_idx = lax.broadcasted_iota(jnp.int32, (CHUNK, CHUNK), 0)
    s_idx = lax.broadcasted_iota(jnp.int32, (CHUNK, CHUNK), 1)
    causal = s_idx <= t_idx
    w_heads = [jnp.where(causal, sw_ref[hh], 0.0).astype(jnp.bfloat16) for hh in range(H_A)]
    for ci in range(SEQ_TILE // CHUNK):
        pos = slice(ci * CHUNK, (ci + 1) * CHUNK)
        for hh in range(H_A):
            lanes = slice(hh * HD_A, (hh + 1) * HD_A)
            mixed = jnp.dot(w_heads[hh], v[pos, lanes], preferred_element_type=jnp.float32) + sb_ref[:, lanes]
            mix_ref[pos, lanes] = (u[pos, lanes] * mixed).astype(jnp.bfloat16)

    o_ref[0] = x + jnp.dot(mix_ref[...], wout_ref[...], preferred_element_type=jnp.float32)


def _mlp_kernel(x_ref, n2g_ref, w1_ref, w2_ref, fg_ref, o_ref, *, apply_final_norm):
    x = x_ref[...]
    hb = _rms_norm(x, n2g_ref[...]).astype(jnp.bfloat16)
    acc = x
    for c in range(D_FF // FF_CHUNK):
        cols = slice(c * FF_CHUNK, (c + 1) * FF_CHUNK)
        f = jnp.dot(hb, w1_ref[:, cols], preferred_element_type=jnp.float32)
        f = jnp.square(jnp.maximum(f, 0.0)).astype(jnp.bfloat16)
        acc = acc + jnp.dot(f, w2_ref[cols, :], preferred_element_type=jnp.float32)
    if apply_final_norm:
        acc = _rms_norm(acc, fg_ref[...])
    o_ref[...] = acc


def _resident(shape):
    return pl.BlockSpec(shape, lambda *_: (0,) * len(shape), pipeline_mode=pl.Buffered(1))


def _mixer(x, n1g, win, lng, lnb, sw, sb, cw, cb, clg, clb, wout):
    bsz, seq, _ = x.shape
    assert seq % SEQ_TILE == 0 and SEQ_TILE % CHUNK == 0 and SEQ_TILE % CONV_ROWS == 0
    tile = pl.BlockSpec((1, SEQ_TILE, D_MODEL), lambda b, j: (b, j, 0))
    return pl.pallas_call(
        _mixer_kernel,
        out_shape=jax.ShapeDtypeStruct(x.shape, x.dtype),
        grid=(bsz, seq // SEQ_TILE),
        in_specs=[tile, _resident(n1g.shape), _resident(win.shape), _resident(lng.shape), _resident(lnb.shape),
                  _resident(sw.shape), _resident(sb.shape), _resident(cw.shape), _resident(cb.shape),
                  _resident(clg.shape), _resident(clb.shape), _resident(wout.shape)],
        out_specs=tile,
        scratch_shapes=[pltpu.VMEM((H_B, HALO + SEQ_TILE, HD_B), jnp.float32),
                        pltpu.VMEM((SEQ_TILE, D_A + D_B), jnp.bfloat16)],
        compiler_params=pltpu.CompilerParams(
            dimension_semantics=("arbitrary", "arbitrary"), vmem_limit_bytes=VMEM_LIMIT_BYTES),
        name="mixer",
    )(x, n1g, win, lng, lnb, sw, sb, cw, cb, clg, clb, wout)


def _mlp(x2d, n2g, w1, w2, fg, apply_final_norm):
    tokens = x2d.shape[0]
    assert tokens % MLP_TILE == 0
    tile = pl.BlockSpec((MLP_TILE, D_MODEL), lambda i: (i, 0))
    return pl.pallas_call(
        functools.partial(_mlp_kernel, apply_final_norm=apply_final_norm),
        out_shape=jax.ShapeDtypeStruct(x2d.shape, x2d.dtype),
        grid=(tokens // MLP_TILE,),
        in_specs=[tile, _resident(n2g.shape), _resident(w1.shape), _resident(w2.shape), _resident(fg.shape)],
        out_specs=tile,
        compiler_params=pltpu.CompilerParams(
            dimension_semantics=("arbitrary",), vmem_limit_bytes=VMEM_LIMIT_BYTES),
        name="mlp",
    )(x2d, n2g, w1, w2, fg)


def kernel(x, norm1_g, w_in, sgu_ln_g, sgu_ln_b, sgu_w, sgu_b, conv_w, conv_b, conv_ln_g, conv_ln_b,
           w_out, norm2_g, w_ff1, w_ff2, final_g):
    bsz, seq, d = x.shape
    depth = w_in.shape[0]
    bf16 = jnp.bfloat16
    row = lambda p: p.reshape(1, -1)
    for l in range(depth):
        sb = jnp.repeat(jnp.transpose(sgu_b[l]), HD_A, axis=1)
        x = _mixer(x, row(norm1_g[l]), w_in[l].astype(bf16), row(sgu_ln_g[l]), row(sgu_ln_b[l]),
                   sgu_w[l], sb, conv_w[l], row(conv_b[l]), row(conv_ln_g[l]), row(conv_ln_b[l]),
                   w_out[l].astype(bf16))
        x = _mlp(x.reshape(bsz * seq, d), row(norm2_g[l]), w_ff1[l].astype(bf16), w_ff2[l].astype(bf16),
                 row(final_g), apply_final_norm=(l == depth - 1)).reshape(bsz, seq, d)
    return x
```

```python
import functools
import math

import jax
import jax.numpy as jnp
from jax import lax
from jax.experimental import pallas as pl
from jax.experimental.pallas import tpu as pltpu

D_MODEL = 1024
D_A = 512
D_B = 512
CHUNK = 128
H_A = 4
HD_A = D_A // H_A
H_B = 4
HD_B = D_B // H_B
CONV_W = 31
D_FF = 4 * D_MODEL
EPS = 1e-6

SEQ_TILE = 512
HALO = 32
CONV_ROWS = 128
MLP_TILE = 512
FF_CHUNK = 1024
VMEM_LIMIT_BYTES = 56 * 1024 * 1024

_SQRT_HALF = math.sqrt(0.5)


def _rms_norm(x, g):
    ms = jnp.mean(x * x, axis=-1, keepdims=True)
    return x * lax.rsqrt(ms + EPS) * g


def _layer_norm(x, g, b):
    mu = jnp.mean(x, axis=-1, keepdims=True)
    xc = x - mu
    var = jnp.mean(xc * xc, axis=-1, keepdims=True)
    return xc * lax.rsqrt(var + EPS) * g + b


def _gelu(x):
    return 0.5 * x * (1.0 + lax.erf(x * _SQRT_HALF))


def _mixer_kernel(x_ref, n1g_ref, win_ref, lng_ref, lnb_ref, sw_ref, sb_ref,
                  cw_ref, cb_ref, clg_ref, clb_ref, wout_ref, o_ref,
                  g_ref, c_ref, mix_ref):
    @pl.when(pl.program_id(1) == 0)
    def _():
        g_ref[:, 0:HALO, :] = jnp.zeros((H_B, HALO, HD_B), jnp.float32)

    x = x_ref[0]
    h = _rms_norm(x, n1g_ref[...])
    proj = jnp.dot(h.astype(jnp.bfloat16), win_ref[...], preferred_element_type=jnp.float32)

    val = proj[:, 2 * D_A:2 * D_A + D_B]
    gate = proj[:, 2 * D_A + D_B:]
    g = val * jax.nn.sigmoid(gate)
    for grp in range(H_B):
        g_ref[grp, HALO:HALO + SEQ_TILE, :] = g[:, grp * HD_B:(grp + 1) * HD_B]

    for grp in range(H_B):
        lanes = slice(grp * HD_B, (grp + 1) * HD_B)
        taps = [cw_ref[k:k + 1, lanes] for k in range(CONV_W)]
        bias = cb_ref[:, lanes]

        def conv_rows(r, carry, grp=grp, taps=taps, bias=bias):
            base = pl.multiple_of(r * CONV_ROWS, CONV_ROWS)
            acc = jnp.broadcast_to(bias, (CONV_ROWS, HD_B))
            for k in range(CONV_W):
                start = base + (HALO - (CONV_W - 1) + k)
                acc = acc + taps[k] * g_ref[grp, pl.ds(start, CONV_ROWS), :]
            c_ref[grp, pl.ds(base, CONV_ROWS), :] = acc
            return carry

        lax.fori_loop(0, SEQ_TILE // CONV_ROWS, conv_rows, 0)

    for grp in range(H_B):
        lanes = slice(grp * HD_B, (grp + 1) * HD_B)
        c = _layer_norm(c_ref[grp], clg_ref[:, lanes], clb_ref[:, lanes])
        mix_ref[:, D_A + grp * HD_B:D_A + (grp + 1) * HD_B] = (c * jax.nn.sigmoid(c)).astype(jnp.bfloat16)

    g_ref[:, 0:HALO, :] = g_ref[:, SEQ_TILE:SEQ_TILE + HALO, :]

    u = _gelu(proj[:, :D_A])
    v = _layer_norm(_gelu(proj[:, D_A:2 * D_A]), lng_ref[...], lnb_ref[...]).astype(jnp.bfloat16)
    t_idx = lax.broadcasted_iota(jnp.int32, (CHUNK, CHUNK), 0)
    s_idx = lax.broadcasted_iota(jnp.int32, (CHUNK, CHUNK), 1)
    causal = s_idx <= t_idx
    w_heads = [jnp.where(causal, sw_ref[hh], 0.0).astype(jnp.bfloat16) for hh in range(H_A)]
    for ci in range(SEQ_TILE // CHUNK):
        pos = slice(ci * CHUNK, (ci + 1) * CHUNK)
        for hh in range(H_A):
            lanes = slice(hh * HD_A, (hh + 1) * HD_A)
            mixed = jnp.dot(w_heads[hh], v[pos, lanes], preferred_element_type=jnp.float32) + sb_ref[:, lanes]
            mix_ref[pos, lanes] = (u[pos, lanes] * mixed).astype(jnp.bfloat16)

    o_ref[0] = x + jnp.dot(mix_ref[...], wout_ref[...], preferred_element_type=jnp.float32)


def _mlp_kernel(x_ref, n2g_ref, w1_ref, w2_ref, fg_ref, o_ref, *, apply_final_norm):
    x = x_ref[...]
    hb = _rms_norm(x, n2g_ref[...]).astype(jnp.bfloat16)
    acc = x
    for c in range(D_FF // FF_CHUNK):
        cols = slice(c * FF_CHUNK, (c + 1) * FF_CHUNK)
        f = jnp.dot(hb, w1_ref[:, cols], preferred_element_type=jnp.float32)
        f = jnp.square(jnp.maximum(f, 0.0)).astype(jnp.bfloat16)
        acc = acc + jnp.dot(f, w2_ref[cols, :], preferred_element_type=jnp.float32)
    if apply_final_norm:
        acc = _rms_norm(acc, fg_ref[...])
    o_ref[...] = acc


def _resident(shape):
    return pl.BlockSpec(shape, lambda *_: (0,) * len(shape), pipeline_mode=pl.Buffered(1))


def _mixer(x, n1g, win, lng, lnb, sw, sb, cw, cb, clg, clb, wout):
    bsz, seq, _ = x.shape
    assert seq % SEQ_TILE == 0 and SEQ_TILE % CHUNK == 0 and SEQ_TILE % CONV_ROWS == 0
    tile = pl.BlockSpec((1, SEQ_TILE, D_MODEL), lambda b, j: (b, j, 0))
    return pl.pallas_call(
        _mixer_kernel,
        out_shape=jax.ShapeDtypeStruct(x.shape, x.dtype),
        grid=(bsz, seq // SEQ_TILE),
        in_specs=[tile, _resident(n1g.shape), _resident(win.shape), _resident(lng.shape), _resident(lnb.shape),
                  _resident(sw.shape), _resident(sb.shape), _resident(cw.shape), _resident(cb.shape),
                  _resident(clg.shape), _resident(clb.shape), _resident(wout.shape)],
        out_specs=tile,
        scratch_shapes=[pltpu.VMEM((H_B, HALO + SEQ_TILE, HD_B), jnp.float32),
                        pltpu.VMEM((H_B, SEQ_TILE, HD_B), jnp.float32),
                        pltpu.VMEM((SEQ_TILE, D_A + D_B), jnp.bfloat16)],
        compiler_params=pltpu.CompilerParams(
            dimension_semantics=("arbitrary", "arbitrary"), vmem_limit_bytes=VMEM_LIMIT_BYTES),
        name="mixer",
    )(x, n1g, win, lng, lnb, sw, sb, cw, cb, clg, clb, wout)


def _mlp(x2d, n2g, w1, w2, fg, apply_final_norm):
    tokens = x2d.shape[0]
    assert tokens % MLP_TILE == 0
    tile = pl.BlockSpec((MLP_TILE, D_MODEL), lambda i: (i, 0))
    return pl.pallas_call(
        functools.partial(_mlp_kernel, apply_final_norm=apply_final_norm),
        out_shape=jax.ShapeDtypeStruct(x2d.shape, x2d.dtype),
        grid=(tokens // MLP_TILE,),
        in_specs=[tile, _resident(n2g.shape), _resident(w1.shape), _resident(w2.shape), _resident(fg.shape)],
        out_specs=tile,
        compiler_params=pltpu.CompilerParams(
            dimension_semantics=("arbitrary",), vmem_limit_bytes=VMEM_LIMIT_BYTES),
        name="mlp",
    )(x2d, n2g, w1, w2, fg)


def kernel(x, norm1_g, w_in, sgu_ln_g, sgu_ln_b, sgu_w, sgu_b, conv_w, conv_b, conv_ln_g, conv_ln_b,
           w_out, norm2_g, w_ff1, w_ff2, final_g):
    bsz, seq, d = x.shape
    depth = w_in.shape[0]
    bf16 = jnp.bfloat16
    row = lambda p: p.reshape(1, -1)
    for l in range(depth):
        sb = jnp.repeat(jnp.transpose(sgu_b[l]), HD_A, axis=1)
        x = _mixer(x, row(norm1_g[l]), w_in[l].astype(bf16), row(sgu_ln_g[l]), row(sgu_ln_b[l]),
                   sgu_w[l], sb, conv_w[l], row(conv_b[l]), row(conv_ln_g[l]), row(conv_ln_b[l]),
                   w_out[l].astype(bf16))
        x = _mlp(x.reshape(bsz * seq, d), row(norm2_g[l]), w_ff1[l].astype(bf16), w_ff2[l].astype(bf16),
                 row(final_g), apply_final_norm=(l == depth - 1)).reshape(bsz, seq, d)
    return x
```
